```python
import jax, jax.numpy as jnp
from jax import lax
import numpy as np

D_MODEL = 2048
BATCH = 2
SEQ = 4096
DEPTH = 1
DEC_BATCH = 32
DEC_SEQ = 8
PAST_LEN = 16384
PAGE_SIZE = 128

HEAD_DIM = 128
DIL_GROUPS = ((128, 1), (512, 4), (2048, 16))
N_GROUPS = 3
HEADS_PER_GROUP = 4
ATTN_HEADS = N_GROUPS * HEADS_PER_GROUP
ATTN_WIDTH = ATTN_HEADS * HEAD_DIM
ATTN_OUT_WIDTH = HEADS_PER_GROUP * HEAD_DIM
D_CONV = D_MODEL // 2
CONV_WIDTH = 31
MEM_TOKENS = 256
MEM_HEADS = 4
MEM_HEAD_DIM = D_MODEL // 8
MEM_WIDTH = MEM_HEADS * MEM_HEAD_DIM
D_FF = 11 * D_MODEL // 4
FFN_CONV_WIDTH = 3
N_BRANCHES = 3
IN_SPLIT_SIZES = (D_CONV, D_CONV, ATTN_WIDTH, ATTN_WIDTH, ATTN_WIDTH, MEM_WIDTH, D_MODEL, D_MODEL, D_MODEL)
IN_WIDTH = 2 * D_CONV + 3 * ATTN_WIDTH + MEM_WIDTH + N_BRANCHES * D_MODEL
EPS = 1e-6

kernel_name = 'hybrid_conformer_dilated_memory_decoder_step'


def _rmsnorm(x, g):
    xf = x.astype(jnp.float32)
    y = xf * lax.rsqrt(jnp.mean(xf * xf, axis=-1, keepdims=True) + EPS)
    return (y * g.astype(jnp.float32)).astype(x.dtype)


def _layernorm(x, g, b):
    xf = x.astype(jnp.float32)
    mu = jnp.mean(xf, axis=-1, keepdims=True)
    var = jnp.mean(jnp.square(xf - mu), axis=-1, keepdims=True)
    y = (xf - mu) * lax.rsqrt(var + EPS)
    return (y * g.astype(jnp.float32) + b.astype(jnp.float32)).astype(x.dtype)


def _causal_dwconv(x_ext, w, b):
    y = lax.conv_general_dilated(x_ext, w[:, None, :].astype(x_ext.dtype), window_strides=(1,), padding='VALID',
                                 dimension_numbers=('NWC', 'WIO', 'NWC'), feature_group_count=x_ext.shape[-1])
    return y + b.astype(y.dtype)


def _memory_kv(mem, g_mem, w_mem_k, w_mem_v, g_mk):
    B, M, _ = mem.shape
    hm = _rmsnorm(mem, g_mem)
    k = _rmsnorm((hm @ w_mem_k).reshape(B, M, MEM_HEADS, MEM_HEAD_DIM), g_mk)
    v = (hm @ w_mem_v).reshape(B, M, MEM_HEADS, MEM_HEAD_DIM)
    return k, v


def _dilated_prompt(q, k, v, dil, window):
    B, S, H, E = q.shape
    wd = window // dil
    L = S // dil
    nb = -(-L // wd)
    Lp = nb * wd

    def blocks(t):
        t = t.reshape(B, L, dil, H, E)
        t = jnp.pad(t, ((0, 0), (0, Lp - L), (0, 0), (0, 0), (0, 0)))
        return t.reshape(B, nb, wd, dil, H, E)

    def with_prev(t):
        prev = jnp.pad(t[:, :-1], ((0, 0), (1, 0), (0, 0), (0, 0), (0, 0), (0, 0)))
        return jnp.concatenate([prev, t], axis=2)

    qb = blocks(q)
    kk = with_prev(blocks(k))
    vv = with_prev(blocks(v))
    s = jnp.einsum('bnqrhe,bnkrhe->bnrhqk', qb.astype(jnp.float32), kk.astype(jnp.float32)) * (E ** -0.5)
    qi = jnp.arange(wd)[:, None]
    ki = jnp.arange(2 * wd)[None, :]
    dist = qi + wd - ki
    band = (dist >= 0) & (dist <= wd)
    has_prev = (jnp.arange(nb) > 0)[:, None, None] | (ki >= wd)[None]
    mask = band[None] & has_prev
    s = jnp.where(mask[None, :, None, None], s, -jnp.inf)
    lse = jax.nn.logsumexp(s, axis=-1)
    o = jnp.einsum('bnrhqk,bnkrhe->bnqrhe', jnp.exp(s - lse[..., None]), vv.astype(jnp.float32))
    o = o.reshape(B, Lp, dil, H, E)[:, :L].reshape(B, S, H, E)
    lse = jnp.transpose(lse, (0, 1, 4, 2, 3)).reshape(B, Lp, dil, H)[:, :L].reshape(B, S, H)
    return o, lse


def _dilated_sample(q, kf, vf, dil, window):
    B, T, H, E = q.shape
    Lk = kf.shape[1]
    nk = window // dil + 1
    idx = (Lk - T + jnp.arange(T))[:, None] - dil * jnp.arange(nk)[None, :]
    valid = idx >= 0
    idx = jnp.maximum(idx, 0)
    kg = kf[:, idx]
    vg = vf[:, idx]
    s = jnp.einsum('bthe,btkhe->bthk', q.astype(jnp.float32), kg.astype(jnp.float32)) * (E ** -0.5)
    s = jnp.where(valid[None, :, None, :], s, -jnp.inf)
    lse = jax.nn.logsumexp(s, axis=-1)
    o = jnp.einsum('bthk,btkhe->bthe', jnp.exp(s - lse[..., None]), vg.astype(jnp.float32))
    return o, lse


def _dilated_mixture(q, k, v, win_bufs):
    outs, lses, new_bufs = [], [], []
    for g, (window, dil) in enumerate(DIL_GROUPS):
        sl = slice(g * HEADS_PER_GROUP, (g + 1) * HEADS_PER_GROUP)
        qg, kg, vg = q[:, :, sl], k[:, :, sl], v[:, :, sl]
        if win_bufs is None:
            kf, vf = kg, vg
            o, lse = _dilated_prompt(qg, kg, vg, dil, window)
        else:
            kb, vb = win_bufs[g]
            kf = jnp.concatenate([kb.astype(kg.dtype), kg], axis=1)
            vf = jnp.concatenate([vb.astype(vg.dtype), vg], axis=1)
            o, lse = _dilated_sample(qg, kf, vf, dil, window)
        keep = min(window, kf.shape[1])
        new_bufs += [kf[:, -keep:], vf[:, -keep:]]
        outs.append(o)
        lses.append(lse)
    alpha = jax.nn.softmax(jnp.stack(lses, axis=0), axis=0)
    o = jnp.sum(alpha[..., None] * jnp.stack(outs, axis=0), axis=0)
    return o, new_bufs


def _layer(x, conv_prev, ffn_prev, mem_k, mem_v, win_bufs, p):
    B, T, _ = x.shape
    h = _rmsnorm(x, p['g_mix'])
    z = h @ p['w_in']
    a, b, q, k, v, qm, gc, ga, gm = jnp.split(z, np.cumsum(IN_SPLIT_SIZES)[:-1].tolist(), axis=-1)

    u = a * jax.nn.sigmoid(b)
    u_ext = jnp.concatenate([conv_prev.astype(u.dtype), u], axis=1)
    c = _causal_dwconv(u_ext, p['w_dw'], p['b_dw'])
    c = jax.nn.silu(_layernorm(c, p['g_cln'], p['b_cln']))
    br_conv = c @ p['w_conv_out']
    conv_new = u_ext[:, -(CONV_WIDTH - 1):]

    q = _rmsnorm(q.reshape(B, T, ATTN_HEADS, HEAD_DIM), p['g_q'])
    k = _rmsnorm(k.reshape(B, T, ATTN_HEADS, HEAD_DIM), p['g_k'])
    v = v.reshape(B, T, ATTN_HEADS, HEAD_DIM)
    o_attn, win_new = _dilated_mixture(q, k, v, win_bufs)
    br_attn = o_attn.reshape(B, T, ATTN_OUT_WIDTH).astype(x.dtype) @ p['w_attn_out']

    qm = _rmsnorm(qm.reshape(B, T, MEM_HEADS, MEM_HEAD_DIM), p['g_mq'])
    s = jnp.einsum('bthe,bmhe->bhtm', qm.astype(jnp.float32), mem_k.astype(jnp.float32)) * (MEM_HEAD_DIM ** -0.5)
    o_m = jnp.einsum('bhtm,bmhe->bthe', jax.nn.softmax(s, axis=-1), mem_v.astype(jnp.float32))
    br_mem = o_m.reshape(B, T, MEM_WIDTH).astype(x.dtype) @ p['w_mem_out']

    m = jax.nn.sigmoid(gc) * br_conv + jax.nn.sigmoid(ga) * br_attn + jax.nn.sigmoid(gm) * br_mem
    x = x + m @ p['w_o']

    h2 = _rmsnorm(x, p['g_ffn'])
    up = h2 @ p['w_up']
    up_ext = jnp.concatenate([ffn_prev.astype(up.dtype), up], axis=1)
    c2 = _causal_dwconv(up_ext, p['w_ffn_dw'], p['b_ffn_dw'])
    gate, val = jnp.split(c2, 2, axis=-1)
    x = x + (jax.nn.silu(gate) * val) @ p['w_down']
    ffn_new = up_ext[:, -(FFN_CONV_WIDTH - 1):]
    return x, (conv_new, *win_new, ffn_new)


def setup_inputs(seed: int = 0) -> dict:
    key = jax.random.key(seed)
    ks = iter(jax.random.split(key, 64))

    def nrm(shape, scale):
        return scale * jax.random.normal(next(ks), shape, jnp.float32)

    def gain(shape):
        return 1.0 + nrm(shape, 0.02)

    L = DEPTH
    wb = [min(w, PAST_LEN) for w, _ in DIL_GROUPS]
    return {
        'x_prompt': nrm((BATCH, SEQ, D_MODEL), 1.0),
        'x_sample': nrm((DEC_BATCH, DEC_SEQ, D_MODEL), 1.0),
        'cache_mem_k': nrm((L, DEC_BATCH, MEM_TOKENS, MEM_HEADS, MEM_HEAD_DIM), 1.0),
        'cache_mem_v': nrm((L, DEC_BATCH, MEM_TOKENS, MEM_HEADS, MEM_HEAD_DIM), 1.0),
        'state_conv': nrm((L, DEC_BATCH, CONV_WIDTH - 1, D_CONV), 0.5),
        'state_win1_k': nrm((L, DEC_BATCH, wb[0], HEADS_PER_GROUP, HEAD_DIM), 1.0),
        'state_win1_v': nrm((L, DEC_BATCH, wb[0], HEADS_PER_GROUP, HEAD_DIM), 1.0),
        'state_win2_k': nrm((L, DEC_BATCH, wb[1], HEADS_PER_GROUP, HEAD_DIM), 1.0),
        'state_win2_v': nrm((L, DEC_BATCH, wb[1], HEADS_PER_GROUP, HEAD_DIM), 1.0),
        'state_win3_k': nrm((L, DEC_BATCH, wb[2], HEADS_PER_GROUP, HEAD_DIM), 1.0),
        'state_win3_v': nrm((L, DEC_BATCH, wb[2], HEADS_PER_GROUP, HEAD_DIM), 1.0),
        'state_ffn_conv': nrm((L, DEC_BATCH, FFN_CONV_WIDTH - 1, 2 * D_FF), 1.0),
        'mem_prompt': nrm((BATCH, MEM_TOKENS, D_MODEL), 1.0),
        'g_mix': gain((L, D_MODEL)),
        'w_in': nrm((L, D_MODEL, IN_WIDTH), D_MODEL ** -0.5),
        'w_dw': nrm((L, CONV_WIDTH, D_CONV), CONV_WIDTH ** -0.5),
        'b_dw': nrm((L, D_CONV), 0.02),
        'g_cln': gain((L, D_CONV)),
        'b_cln': nrm((L, D_CONV), 0.02),
        'w_conv_out': nrm((L, D_CONV, D_MODEL), D_CONV ** -0.5),
        'g_q': gain((L, HEAD_DIM)),
        'g_k': gain((L, HEAD_DIM)),
        'w_attn_out': nrm((L, ATTN_OUT_WIDTH, D_MODEL), ATTN_OUT_WIDTH ** -0.5),
        'g_mem': gain((L, D_MODEL)),
        'w_mem_k': nrm((L, D_MODEL, MEM_WIDTH), D_MODEL ** -0.5),
        'w_mem_v': nrm((L, D_MODEL, MEM_WIDTH), D_MODEL ** -0.5),
        'g_mq': gain((L, MEM_HEAD_DIM)),
        'g_mk': gain((L, MEM_HEAD_DIM)),
        'w_mem_out': nrm((L, MEM_WIDTH, D_MODEL), MEM_WIDTH ** -0.5),
        'w_o': nrm((L, D_MODEL, D_MODEL), D_MODEL ** -0.5),
        'g_ffn': gain((L, D_MODEL)),
        'w_up': nrm((L, D_MODEL, 2 * D_FF), D_MODEL ** -0.5),
        'w_ffn_dw': nrm((L, FFN_CONV_WIDTH, 2 * D_FF), FFN_CONV_WIDTH ** -0.5),
        'b_ffn_dw': nrm((L, 2 * D_FF), 0.02),
        'w_down': nrm((L, D_FF, D_MODEL), D_FF ** -0.5),
    }


def reference(x_prompt, x_sample, cache_mem_k, cache_mem_v, state_conv, state_win1_k, state_win1_v,
              state_win2_k, state_win2_v, state_win3_k, state_win3_v, state_ffn_conv, mem_prompt,
              g_mix, w_in, w_dw, b_dw, g_cln, b_cln, w_conv_out, g_q, g_k, w_attn_out,
              g_mem, w_mem_k, w_mem_v, g_mq, g_mk, w_mem_out, w_o, g_ffn, w_up, w_ffn_dw, b_ffn_dw, w_down):
    B = x_prompt.shape[0]
    conv0 = jnp.zeros((B, CONV_WIDTH - 1, D_CONV), x_prompt.dtype)
    ffn0 = jnp.zeros((B, FFN_CONV_WIDTH - 1, 2 * D_FF), x_prompt.dtype)
    xp, xs = x_prompt, x_sample
    new_p, new_s = [], []
    for l in range(DEPTH):
        p = {'g_mix': g_mix[l], 'w_in': w_in[l], 'w_dw': w_dw[l], 'b_dw': b_dw[l], 'g_cln': g_cln[l],
             'b_cln': b_cln[l], 'w_conv_out': w_conv_out[l], 'g_q': g_q[l], 'g_k': g_k[l],
             'w_attn_out': w_attn_out[l], 'g_mq': g_mq[l], 'w_mem_out': w_mem_out[l], 'w_o': w_o[l],
             'g_ffn': g_ffn[l], 'w_up': w_up[l], 'w_ffn_dw': w_ffn_dw[l], 'b_ffn_dw': b_ffn_dw[l],
             'w_down': w_down[l]}
        mk_p, mv_p = _memory_kv(mem_prompt, g_mem[l], w_mem_k[l], w_mem_v[l], g_mk[l])
        xp, st_p = _layer(xp, conv0, ffn0, mk_p, mv_p, None, p)
        bufs = ((state_win1_k[l], state_win1_v[l]), (state_win2_k[l], state_win2_v[l]),
                (state_win3_k[l], state_win3_v[l]))
        xs, st_s = _layer(xs, state_conv[l], state_ffn_conv[l], cache_mem_k[l], cache_mem_v[l], bufs, p)
        new_p.append(st_p + (mk_p, mv_p))
        new_s.append(st_s)
    sp = [jnp.stack(t, axis=0) for t in zip(*new_p)]
    ss = [jnp.stack(t, axis=0) for t in zip(*new_s)]
    return (xp, xs, sp[0], sp[1], sp[2], sp[3], sp[4], sp[5], sp[6], sp[7], sp[8], sp[9],
            ss[0], ss[1], ss[2], ss[3], ss[4], ss[5], ss[6], ss[7])
```

```python
import functools

import jax
import jax.numpy as jnp
from jax import lax
from jax.experimental import pallas as pl
from jax.experimental.pallas import tpu as pltpu

F32 = jnp.float32
BF16 = jnp.bfloat16

HEAD_DIM = 128
DIL_GROUPS = ((128, 1), (512, 4), (2048, 16))
N_GROUPS = len(DIL_GROUPS)
HEADS_PER_GROUP = 4
GROUP_WIDTH = HEADS_PER_GROUP * HEAD_DIM
ATTN_WIDTH = N_GROUPS * GROUP_WIDTH
WINDOW_BLOCK = 128
CONV_WIDTH = 31
MEM_HEADS = 4
FFN_CONV_WIDTH = 3
EPS = 1e-6

SUBLANES = 8
VMEM_LIMIT_BYTES = 56 * 1024 * 1024

_SEQ2 = ("arbitrary", "arbitrary")


def _params(n_axes, vmem=VMEM_LIMIT_BYTES):
    return pltpu.CompilerParams(dimension_semantics=("arbitrary",) * n_axes, vmem_limit_bytes=vmem)


def _rmsnorm_kernel(x_ref, g_ref, o_ref):
    x = x_ref[...]
    ms = jnp.mean(x * x, axis=-1, keepdims=True)
    o_ref[...] = ((x * lax.rsqrt(ms + EPS)) * g_ref[...]).astype(o_ref.dtype)


def _rmsnorm(x, g, tm):
    m, d = x.shape
    return pl.pallas_call(
        _rmsnorm_kernel,
        grid=(m // tm,),
        in_specs=[pl.BlockSpec((tm, d), lambda i: (i, 0)), pl.BlockSpec((1, d), lambda i: (0, 0))],
        out_specs=pl.BlockSpec((tm, d), lambda i: (i, 0)),
        out_shape=jax.ShapeDtypeStruct((m, d), BF16),
        compiler_params=_params(1),
        name="rmsnorm",
    )(x, g.reshape(1, d))


def _cast_weights(w_refs, wb_refs):
    @pl.when(pl.program_id(1) == 0)
    def _():
        for w_ref, wb_ref in zip(w_refs, wb_refs):
            wb_ref[...] = w_ref[...].astype(BF16)


def _mm(x, wb_ref):
    return jnp.dot(x, wb_ref[...], preferred_element_type=F32)


def _proj_plain_kernel(x_ref, w_ref, o_ref, wb_ref):
    _cast_weights([w_ref], [wb_ref])
    o_ref[...] = _mm(x_ref[...], wb_ref).astype(o_ref.dtype)


def _proj_residual_kernel(x_ref, w_ref, r_ref, o_ref, wb_ref):
    _cast_weights([w_ref], [wb_ref])
    o_ref[...] = r_ref[...] + _mm(x_ref[...], wb_ref)


def _proj_glu_kernel(x_ref, wa_ref, wg_ref, o_ref, wab_ref, wgb_ref):
    _cast_weights([wa_ref, wg_ref], [wab_ref, wgb_ref])
    x = x_ref[...]
    o_ref[...] = _mm(x, wab_ref) * jax.nn.sigmoid(_mm(x, wgb_ref))


def _head_rmsnorm_store(acc, g_ref, o_ref, head_dim):
    for h in range(acc.shape[-1] // head_dim):
        sl = slice(h * head_dim, (h + 1) * head_dim)
        blk = acc[:, sl]
        ms = jnp.mean(blk * blk, axis=-1, keepdims=True)
        o_ref[:, sl] = ((blk * lax.rsqrt(ms + EPS)) * g_ref[:, sl]).astype(o_ref.dtype)


def _proj_headnorm_kernel(x_ref, w_ref, g_ref, o_ref, wb_ref, *, head_dim, n_norm_tiles, n_tiles):
    _cast_weights([w_ref], [wb_ref])
    acc = _mm(x_ref[...], wb_ref)
    if n_norm_tiles == n_tiles:
        _head_rmsnorm_store(acc, g_ref, o_ref, head_dim)
    else:
        j = pl.program_id(0)

        @pl.when(j < n_norm_tiles)
        def _():
            _head_rmsnorm_store(acc, g_ref, o_ref, head_dim)

        @pl.when(j >= n_norm_tiles)
        def _():
            o_ref[...] = acc.astype(o_ref.dtype)


def _proj(body, x, weights, col0s, n_out, tm, tn, out_dtype, extras=(), name="proj"):
    m, k = x.shape
    grid = (n_out // tn, m // tm)
    in_specs = [pl.BlockSpec((tm, k), lambda j, i: (i, 0))]
    for col0 in col0s:
        in_specs.append(pl.BlockSpec((k, tn), functools.partial(lambda j, i, c: (0, c + j), c=col0 // tn)))
    for _, block, index_map in extras:
        in_specs.append(pl.BlockSpec(block, index_map))
    return pl.pallas_call(
        body,
        grid=grid,
        in_specs=in_specs,
        out_specs=pl.BlockSpec((tm, tn), lambda j, i: (i, j)),
        out_shape=jax.ShapeDtypeStruct((m, n_out), out_dtype),
        scratch_shapes=[pltpu.VMEM((k, tn), BF16) for _ in weights],
        compiler_params=_params(2),
        name=name,
    )(x, *weights, *[e[0] for e in extras])


def _proj_plain(x, w, col0, n_out, tm, tn, out_dtype=F32):
    return _proj(_proj_plain_kernel, x, [w], [col0], n_out, tm, tn, out_dtype, name="proj_plain")


def _proj_residual(x, w, res, tm, tn):
    n_out = w.shape[1]
    extras = [(res, (tm, tn), lambda j, i: (i, j))]
    return _proj(_proj_residual_kernel, x, [w], [0], n_out, tm, tn, F32, extras, name="proj_residual")


def _proj_glu(x, w, col_a, col_g, n_out, tm, tn):
    return _proj(_proj_glu_kernel, x, [w, w], [col_a, col_g], n_out, tm, tn, F32, name="proj_glu")


def _proj_headnorm(x, w, col0, n_out, gain_row, head_dim, n_norm_tiles, tm, tn, out_dtype):
    body = functools.partial(_proj_headnorm_kernel, head_dim=head_dim, n_norm_tiles=n_norm_tiles,
                             n_tiles=n_out // tn)
    extras = [(gain_row, (1, tn), lambda j, i: (0, j))]
    return _proj(body, x, [w], [col0], n_out, tm, tn, out_dtype, extras, name="proj_headnorm")


CONV_HALO = 32
CONV_PAD = CONV_HALO - (CONV_WIDTH - 1)
CONV_CHUNK = 16


def _ln_swish(acc, g_ref, b_ref):
    mu = jnp.mean(acc, axis=-1, keepdims=True)
    cen = acc - mu
    var = jnp.mean(cen * cen, axis=-1, keepdims=True)
    y = (cen * lax.rsqrt(var + EPS)) * g_ref[...] + b_ref[...]
    return y * jax.nn.sigmoid(y)


def _conv_prompt_kernel(u_ref, halo_ref, w_ref, bdw_ref, g_ref, b_ref, o_ref, ext_ref, sh_ref, *, tr):
    i = pl.program_id(1)
    ext_ref[0:CONV_HALO, :] = jnp.where(i == 0, 0.0, halo_ref[0])
    ext_ref[CONV_HALO:CONV_HALO + tr, :] = u_ref[0]
    sh_ref[0] = ext_ref[...]
    span = tr + CONV_HALO - SUBLANES
    for s in range(1, SUBLANES):
        sh_ref[s, 0:span, :] = ext_ref[s:s + span, :]

    def chunk(c, carry):
        base = pl.multiple_of(c * CONV_CHUNK, CONV_CHUNK)
        acc = jnp.broadcast_to(bdw_ref[...], (CONV_CHUNK, bdw_ref.shape[-1]))
        for j in range(CONV_WIDTH):
            off = CONV_PAD + j
            s = off % SUBLANES
            start = pl.multiple_of(base + (off - s), SUBLANES)
            acc = acc + w_ref[j:j + 1, :] * sh_ref[s, pl.ds(start, CONV_CHUNK), :]
        o_ref[0, pl.ds(base, CONV_CHUNK), :] = _ln_swish(acc, g_ref, b_ref).astype(o_ref.dtype)
        return carry

    lax.fori_loop(0, tr // CONV_CHUNK, chunk, 0)


def _conv_prompt(u, w_dw, b_dw, g_cln, b_cln, tr):
    b, s, c = u.shape
    halo_blocks = tr // CONV_HALO
    row = lambda a: a.reshape(1, c)
    vec = pl.BlockSpec((1, c), lambda bi, i: (0, 0))
    return pl.pallas_call(
        functools.partial(_conv_prompt_kernel, tr=tr),
        grid=(b, s // tr),
        in_specs=[
            pl.BlockSpec((1, tr, c), lambda bi, i: (bi, i, 0)),
            pl.BlockSpec((1, CONV_HALO, c), lambda bi, i: (bi, jnp.maximum(i * halo_blocks - 1, 0), 0)),
            pl.BlockSpec((CONV_WIDTH, c), lambda bi, i: (0, 0)),
            vec, vec, vec,
        ],
        out_specs=pl.BlockSpec((1, tr, c), lambda bi, i: (bi, i, 0)),
        out_shape=jax.ShapeDtypeStruct((b, s, c), BF16),
        scratch_shapes=[pltpu.VMEM((CONV_HALO + tr, c), F32),
                        pltpu.VMEM((SUBLANES, tr + CONV_HALO, c), F32)],
        compiler_params=_params(2),
        name="conv_prompt",
    )(u, u, w_dw, row(b_dw), row(g_cln), row(b_cln))


def _conv_sample_kernel(u_ref, st_ref, w_ref, bdw_ref, g_ref, b_ref, o_ref, ext_ref, *, nb, t):
    for bi in range(nb):
        ext_ref[0:CONV_HALO, :] = st_ref[bi]
        ext_ref[CONV_HALO:CONV_HALO + t, :] = u_ref[bi]
        acc = jnp.broadcast_to(bdw_ref[...], (t, bdw_ref.shape[-1]))
        for j in range(CONV_WIDTH):
            acc = acc + w_ref[j:j + 1, :] * ext_ref[CONV_PAD + j:CONV_PAD + j + t, :]
        o_ref[bi] = _ln_swish(acc, g_ref, b_ref).astype(o_ref.dtype)


def _conv_sample(u, state_padded, w_dw, b_dw, g_cln, b_cln, nb):
    b, t, c = u.shape
    row = lambda a: a.reshape(1, c)
    vec = pl.BlockSpec((1, c), lambda i: (0, 0))
    return pl.pallas_call(
        functools.partial(_conv_sample_kernel, nb=nb, t=t),
        grid=(b // nb,),
        in_specs=[
            pl.BlockSpec((nb, t, c), lambda i: (i, 0, 0)),
            pl.BlockSpec((nb, CONV_HALO, c), lambda i: (i, 0, 0)),
            pl.BlockSpec((CONV_WIDTH, c), lambda i: (0, 0)),
            vec, vec, vec,
        ],
        out_specs=pl.BlockSpec((nb, t, c), lambda i: (i, 0, 0)),
        out_shape=jax.ShapeDtypeStruct((b, t, c), F32),
        scratch_shapes=[pltpu.VMEM((CONV_HALO + t, c), F32)],
        compiler_params=_params(1),
        name="conv_sample",
    )(u, state_padded, w_dw, row(b_dw), row(g_cln), row(b_cln))


def _attn_prompt_kernel(q_ref, kc_ref, kp_ref, vc_ref, vp_ref, o_ref, lse_ref):
    n = pl.program_id(1)
    wd = WINDOW_BLOCK
    qi = lax.broadcasted_iota(jnp.int32, (wd, wd), 0)
    ki = lax.broadcasted_iota(jnp.int32, (wd, wd), 1)
    mask_cur = ki <= qi
    mask_prev = (ki >= qi) & (n > 0)
    scale = HEAD_DIM ** -0.5
    nt = (((1,), (1,)), ((), ()))
    for h in range(HEADS_PER_GROUP):
        sl = slice(h * HEAD_DIM, (h + 1) * HEAD_DIM)
        q = q_ref[0, :, sl].astype(BF16)
        s_cur = lax.dot_general(q, kc_ref[0, :, sl].astype(BF16), nt, preferred_element_type=F32) * scale
        s_prev = lax.dot_general(q, kp_ref[0, :, sl].astype(BF16), nt, preferred_element_type=F32) * scale
        s_cur = jnp.where(mask_cur, s_cur, -jnp.inf)
        s_prev = jnp.where(mask_prev, s_prev, -jnp.inf)
        m = jnp.maximum(jnp.max(s_cur, axis=-1, keepdims=True), jnp.max(s_prev, axis=-1, keepdims=True))
        e_cur = jnp.exp(s_cur - m)
        e_prev = jnp.exp(s_prev - m)
        lse = m + jnp.log(jnp.sum(e_cur, axis=-1, keepdims=True) + jnp.sum(e_prev, axis=-1, keepdims=True))
        p_cur = jnp.exp(s_cur - lse).astype(BF16)
        p_prev = jnp.exp(s_prev - lse).astype(BF16)
        o = jnp.dot(p_cur, vc_ref[0, :, sl].astype(BF16), preferred_element_type=F32)
        o = o + jnp.dot(p_prev, vp_ref[0, :, sl].astype(BF16), preferred_element_type=F32)
        o_ref[0, :, sl] = o
        lse_ref[0, :, sl] = jnp.broadcast_to(lse, (wd, HEAD_DIM))


def _attn_prompt_group(qkv, g, dil):
    b, s, width = qkv.shape
    sub_len = s // dil
    nb = sub_len // WINDOW_BLOCK
    cols_per_row = width // GROUP_WIDTH
    view = qkv.reshape(b, sub_len, dil * width)

    def col(which):
        return which * N_GROUPS + g

    def cur(which):
        return pl.BlockSpec((1, WINDOW_BLOCK, GROUP_WIDTH),
                            lambda bi, n, r: (bi, n, r * cols_per_row + col(which)))

    def prev(which):
        return pl.BlockSpec((1, WINDOW_BLOCK, GROUP_WIDTH),
                            lambda bi, n, r: (bi, jnp.maximum(n - 1, 0), r * cols_per_row + col(which)))

    out_spec = pl.BlockSpec((1, WINDOW_BLOCK, GROUP_WIDTH), lambda bi, n, r: (bi, n, r))
    out_shape = jax.ShapeDtypeStruct((b, sub_len, dil * GROUP_WIDTH), F32)
    o, lse = pl.pallas_call(
        _attn_prompt_kernel,
        grid=(b, nb, dil),
        in_specs=[cur(0), cur(1), prev(1), cur(2), prev(2)],
        out_specs=[out_spec, out_spec],
        out_shape=[out_shape, out_shape],
        compiler_params=_params(3),
        name=f"attn_prompt_g{g}",
    )(view, view, view, view, view)
    return o.reshape(b, s, GROUP_WIDTH), lse.reshape(b, s, GROUP_WIDTH)


def _mix_kernel(o0_ref, o1_ref, o2_ref, l0_ref, l1_ref, l2_ref, out_ref):
    l0, l1, l2 = l0_ref[...], l1_ref[...], l2_ref[...]
    m = jnp.maximum(jnp.maximum(l0, l1), l2)
    e0, e1, e2 = jnp.exp(l0 - m), jnp.exp(l1 - m), jnp.exp(l2 - m)
    den = e0 + e1 + e2
    out = (e0 / den) * o0_ref[...] + (e1 / den) * o1_ref[...] + (e2 / den) * o2_ref[...]
    out_ref[...] = out.astype(out_ref.dtype)


def _mix_groups(outs, lses, tm):
    m, w = outs[0].shape
    spec = pl.BlockSpec((tm, w), lambda i: (i, 0))
    return pl.pallas_call(
        _mix_kernel,
        grid=(m // tm,),
        in_specs=[spec] * 6,
        out_specs=spec,
        out_shape=jax.ShapeDtypeStruct((m, w), BF16),
        compiler_params=_params(1),
        name="mix_groups",
    )(*outs, *lses)


def _attn_sample_kernel(qkv_ref, *refs, t):
    state_refs = refs[:2 * N_GROUPS]
    o_ref = refs[2 * N_GROUPS]
    new_refs = refs[2 * N_GROUPS + 1:]
    scale = HEAD_DIM ** -0.5
    nt = (((1,), (1,)), ((), ()))
    outs, lses = [], []
    for g, (window, dil) in enumerate(DIL_GROUPS):
        ks_ref, vs_ref = state_refs[2 * g], state_refs[2 * g + 1]
        kn_ref, vn_ref = new_refs[2 * g], new_refs[2 * g + 1]
        lb = ks_ref.shape[1]
        k_new = qkv_ref[0, :, ATTN_WIDTH + g * GROUP_WIDTH:ATTN_WIDTH + (g + 1) * GROUP_WIDTH]
        v_new = qkv_ref[0, :, 2 * ATTN_WIDTH + g * GROUP_WIDTH:2 * ATTN_WIDTH + (g + 1) * GROUP_WIDTH]
        kn_ref[0, 0:lb - t, :] = ks_ref[0, t:lb, :]
        kn_ref[0, lb - t:lb, :] = k_new
        vn_ref[0, 0:lb - t, :] = vs_ref[0, t:lb, :]
        vn_ref[0, lb - t:lb, :] = v_new
        ti_s = lax.broadcasted_iota(jnp.int32, (t, lb), 0)
        c_s = lax.broadcasted_iota(jnp.int32, (t, lb), 1)
        delta_s = lb + ti_s - c_s
        mask_s = ((delta_s & (dil - 1)) == 0) & (delta_s <= window)
        ti_n = lax.broadcasted_iota(jnp.int32, (t, t), 0)
        c_n = lax.broadcasted_iota(jnp.int32, (t, t), 1)
        delta_n = ti_n - c_n
        mask_n = (delta_n >= 0) & ((delta_n & (dil - 1)) == 0)
        o_heads, lse_heads = [], []
        for h in range(HEADS_PER_GROUP):
            sl = slice(h * HEAD_DIM, (h + 1) * HEAD_DIM)
            q = qkv_ref[0, :, g * GROUP_WIDTH + h * HEAD_DIM:g * GROUP_WIDTH + (h + 1) * HEAD_DIM].astype(BF16)
            s_s = lax.dot_general(q, ks_ref[0, :, sl].astype(BF16), nt, preferred_element_type=F32) * scale
            s_n = lax.dot_general(q, k_new[:, sl].astype(BF16), nt, preferred_element_type=F32) * scale
            s_s = jnp.where(mask_s, s_s, -jnp.inf)
            s_n = jnp.where(mask_n, s_n, -jnp.inf)
            m = jnp.maximum(jnp.max(s_s, axis=-1, keepdims=True), jnp.max(s_n, axis=-1, keepdims=True))
            l = (jnp.sum(jnp.exp(s_s - m), axis=-1, keepdims=True)
                 + jnp.sum(jnp.exp(s_n - m), axis=-1, keepdims=True))
            lse = m + jnp.log(l)
            p_s = jnp.exp(s_s - lse).astype(BF16)
            p_n = jnp.exp(s_n - lse).astype(BF16)
            o = jnp.dot(p_s, vs_ref[0, :, sl].astype(BF16), preferred_element_type=F32)
            o = o + jnp.dot(p_n, v_new[:, sl].astype(BF16), preferred_element_type=F32)
            o_heads.append(o)
            lse_heads.append(lse)
        outs.append(o_heads)
        lses.append(lse_heads)
    for h in range(HEADS_PER_GROUP):
        l0, l1, l2 = lses[0][h], lses[1][h], lses[2][h]
        m = jnp.maximum(jnp.maximum(l0, l1), l2)
        e0, e1, e2 = jnp.exp(l0 - m), jnp.exp(l1 - m), jnp.exp(l2 - m)
        den = e0 + e1 + e2
        mixed = (e0 / den) * outs[0][h] + (e1 / den) * outs[1][h] + (e2 / den) * outs[2][h]
        o_ref[0, :, h * HEAD_DIM:(h + 1) * HEAD_DIM] = mixed.astype(o_ref.dtype)


def _attn_sample(qkv, states):
    b, t, width = qkv.shape
    state_specs = [pl.BlockSpec((1,) + s.shape[1:], lambda bi: (bi, 0, 0)) for s in states]
    outs = pl.pallas_call(
        functools.partial(_attn_sample_kernel, t=t),
        grid=(b,),
        in_specs=[pl.BlockSpec((1, t, width), lambda bi: (bi, 0, 0))] + state_specs,
        out_specs=[pl.BlockSpec((1, t, GROUP_WIDTH), lambda bi: (bi, 0, 0))] + state_specs,
        out_shape=[jax.ShapeDtypeStruct((b, t, GROUP_WIDTH), F32)]
        + [jax.ShapeDtypeStruct(s.shape, F32) for s in states],
        compiler_params=_params(1, vmem=60 * 1024 * 1024),
        name="attn_sample",
    )(qkv, *states)
    return outs[0], outs[1:]


def _mem_attn_kernel(q_ref, k_ref, v_ref, o_ref, *, head_dim):
    scale = head_dim ** -0.5
    nt = (((1,), (1,)), ((), ()))
    for h in range(MEM_HEADS):
        sl = slice(h * head_dim, (h + 1) * head_dim)
        q = q_ref[0, :, sl].astype(BF16)
        s = lax.dot_general(q, k_ref[0, :, sl].astype(BF16), nt, preferred_element_type=F32) * scale
        m = jnp.max(s, axis=-1, keepdims=True)
        e = jnp.exp(s - m)
        p = (e / jnp.sum(e, axis=-1, keepdims=True)).astype(BF16)
        o = jnp.dot(p, v_ref[0, :, sl].astype(BF16), preferred_element_type=F32)
        o_ref[0, :, sl] = o.astype(o_ref.dtype)


def _mem_attn(qm, mem_k, mem_v, tq, out_dtype):
    b, t, w = qm.shape
    mt = mem_k.shape[1]
    kv_spec = pl.BlockSpec((1, mt, w), lambda bi, i: (bi, 0, 0))
    return pl.pallas_call(
        functools.partial(_mem_attn_kernel, head_dim=w // MEM_HEADS),
        grid=(b, t // tq),
        in_specs=[pl.BlockSpec((1, tq, w), lambda bi, i: (bi, i, 0)), kv_spec, kv_spec],
        out_specs=pl.BlockSpec((1, tq, w), lambda bi, i: (bi, i, 0)),
        out_shape=jax.ShapeDtypeStruct((b, t, w), out_dtype),
        compiler_params=_params(2),
        name="mem_attn",
    )(qm, mem_k, mem_v)


def _merge_kernel(h_ref, c_ref, a_ref, mm_ref, wgc_ref, wga_ref, wgm_ref, wc_ref, wa_ref, wm_ref, o_ref,
                  *wb_refs):
    _cast_weights([wgc_ref, wga_ref, wgm_ref, wc_ref, wa_ref, wm_ref], wb_refs)
    h = h_ref[...]
    out = jax.nn.sigmoid(_mm(h, wb_refs[0])) * _mm(c_ref[...], wb_refs[3])
    out = out + jax.nn.sigmoid(_mm(h, wb_refs[1])) * _mm(a_ref[...], wb_refs[4])
    out = out + jax.nn.sigmoid(_mm(h, wb_refs[2])) * _mm(mm_ref[...], wb_refs[5])
    o_ref[...] = out.astype(o_ref.dtype)


def _merge(h, c, a, mo, w_in, gate_col0, w_conv_out, w_attn_out, w_mem_out, tm, tn):
    m, d = h.shape
    n_out = w_conv_out.shape[1]
    lhs = lambda arr: pl.BlockSpec((tm, arr.shape[1]), lambda j, i: (i, 0))
    gate = lambda idx: pl.BlockSpec((d, tn), functools.partial(lambda j, i, c0: (0, c0 + j),
                                                               c0=(gate_col0 + idx * n_out) // tn))
    wout = lambda arr: pl.BlockSpec((arr.shape[0], tn), lambda j, i: (0, j))
    return pl.pallas_call(
        _merge_kernel,
        grid=(n_out // tn, m // tm),
        in_specs=[lhs(h), lhs(c), lhs(a), lhs(mo), gate(0), gate(1), gate(2),
                  wout(w_conv_out), wout(w_attn_out), wout(w_mem_out)],
        out_specs=pl.BlockSpec((tm, tn), lambda j, i: (i, j)),
        out_shape=jax.ShapeDtypeStruct((m, n_out), BF16),
        scratch_shapes=[pltpu.VMEM((d, tn), BF16)] * 3
        + [pltpu.VMEM((w.shape[0], tn), BF16) for w in (w_conv_out, w_attn_out, w_mem_out)],
        compiler_params=_params(2),
        name="merge",
    )(h, c, a, mo, w_in, w_in, w_in, w_conv_out, w_attn_out, w_mem_out)


def _ffn_conv_act(ug, uv, pg, pv, wdw_g_ref, wdw_v_ref, bg_ref, bv_ref):
    tm, tn = ug.shape
    groups = tm // SUBLANES
    sub = lax.broadcasted_iota(jnp.int32, (groups, SUBLANES, tn), 1)

    def conv(u, p, w_ref, b_ref):
        u3 = u.reshape(groups, SUBLANES, tn)
        p3 = p.reshape(groups, SUBLANES, tn)
        back1 = jnp.where(sub >= 1, pltpu.roll(u3, 1, axis=1), pltpu.roll(p3, 1, axis=1))
        back2 = jnp.where(sub >= 2, pltpu.roll(u3, 2, axis=1), pltpu.roll(p3, 2, axis=1))
        y = w_ref[2:3, :] * u3 + w_ref[1:2, :] * back1 + w_ref[0:1, :] * back2 + b_ref[...]
        return y.reshape(tm, tn)

    gate = conv(ug, pg, wdw_g_ref, bg_ref)
    val = conv(uv, pv, wdw_v_ref, bv_ref)
    return (gate * jax.nn.sigmoid(gate)) * val


def _ffn_up_prompt_kernel(h_ref, wg_ref, wv_ref, dwg_ref, dwv_ref, bg_ref, bv_ref, act_ref, last_ref,
                          wgb_ref, wvb_ref, carry_ref, *, tiles_per_seq):
    _cast_weights([wg_ref, wv_ref], [wgb_ref, wvb_ref])
    i = pl.program_id(1)
    h = h_ref[...]
    ug = _mm(h, wgb_ref)
    uv = _mm(h, wvb_ref)
    first = (i % tiles_per_seq) == 0
    tail_g = jnp.where(first, 0.0, carry_ref[0])
    tail_v = jnp.where(first, 0.0, carry_ref[1])
    pg = jnp.concatenate([tail_g, ug[:-SUBLANES]], axis=0)
    pv = jnp.concatenate([tail_v, uv[:-SUBLANES]], axis=0)
    carry_ref[0] = ug[-SUBLANES:]
    carry_ref[1] = uv[-SUBLANES:]
    last_ref[0, 0] = ug[-SUBLANES:]
    last_ref[0, 1] = uv[-SUBLANES:]
    act_ref[...] = _ffn_conv_act(ug, uv, pg, pv, dwg_ref, dwv_ref, bg_ref, bv_ref).astype(act_ref.dtype)


def _ffn_up_sample_kernel(h_ref, wg_ref, wv_ref, dwg_ref, dwv_ref, bg_ref, bv_ref, pg_ref, pv_ref,
                          act_ref, up_ref, wgb_ref, wvb_ref):
    _cast_weights([wg_ref, wv_ref], [wgb_ref, wvb_ref])
    h = h_ref[...]
    ug = _mm(h, wgb_ref)
    uv = _mm(h, wvb_ref)
    up_ref[0] = ug
    up_ref[1] = uv
    act_ref[...] = _ffn_conv_act(ug, uv, pg_ref[...], pv_ref[...], dwg_ref, dwv_ref, bg_ref,
                                 bv_ref).astype(act_ref.dtype)


def _ffn_up(h2, w_up, w_dw, b_dw, tm, tn, *, tiles_per_seq=None, prev_rows=None):
    m, d = h2.shape
    d_ff = w_up.shape[1] // 2
    n_j = d_ff // tn
    grid = (n_j, m // tm)
    b_row = b_dw.reshape(1, 2 * d_ff)
    lhs = pl.BlockSpec((tm, d), lambda j, i: (i, 0))
    wg = pl.BlockSpec((d, tn), lambda j, i: (0, j))
    wv = pl.BlockSpec((d, tn), lambda j, i: (0, n_j + j))
    dwg = pl.BlockSpec((FFN_CONV_WIDTH, tn), lambda j, i: (0, j))
    dwv = pl.BlockSpec((FFN_CONV_WIDTH, tn), lambda j, i: (0, n_j + j))
    bg = pl.BlockSpec((1, tn), lambda j, i: (0, j))
    bv = pl.BlockSpec((1, tn), lambda j, i: (0, n_j + j))
    act_spec = pl.BlockSpec((tm, tn), lambda j, i: (i, j))
    act_shape = jax.ShapeDtypeStruct((m, d_ff), BF16)
    w_scratch = [pltpu.VMEM((d, tn), BF16), pltpu.VMEM((d, tn), BF16)]
    if prev_rows is None:
        n_i = m // tm
        return pl.pallas_call(
            functools.partial(_ffn_up_prompt_kernel, tiles_per_seq=tiles_per_seq),
            grid=grid,
            in_specs=[lhs, wg, wv, dwg, dwv, bg, bv],
            out_specs=[act_spec, pl.BlockSpec((1, 2, SUBLANES, tn), lambda j, i: (i, 0, 0, j))],
            out_shape=[act_shape, jax.ShapeDtypeStruct((n_i, 2, SUBLANES, d_ff), F32)],
            scratch_shapes=w_scratch + [pltpu.VMEM((2, SUBLANES, tn), F32)],
            compiler_params=_params(2),
            name="ffn_up_prompt",
        )(h2, w_up, w_up, w_dw, w_dw, b_row, b_row)
    pg = pl.BlockSpec((tm, tn), lambda j, i: (i, j))
    pv = pl.BlockSpec((tm, tn), lambda j, i: (i, n_j + j))
    return pl.pallas_call(
        _ffn_up_sample_kernel,
        grid=grid,
        in_specs=[lhs, wg, wv, dwg, dwv, bg, bv, pg, pv],
        out_specs=[act_spec, pl.BlockSpec((2, tm, tn), lambda j, i: (0, i, j))],
        out_shape=[act_shape, jax.ShapeDtypeStruct((2, m, d_ff), F32)],
        scratch_shapes=w_scratch,
        compiler_params=_params(2),
        name="ffn_up_sample",
    )(h2, w_up, w_up, w_dw, w_dw, b_row, b_row, prev_rows, prev_rows)


def _largest_tile(m, cap):
    tile = min(m, cap)
    while m % tile:
        tile //= 2
    return tile


def _layer(x, p, *, mem_k, mem_v, conv_state=None, win_states=None, ffn_state=None):
    b, t, d = x.shape
    m = b * t
    sample = conv_state is not None
    d_conv = p["w_conv_out"].shape[0]
    mem_width = p["w_mem_out"].shape[0]
    d_ff = p["w_down"].shape[0]
    tm = _largest_tile(m, 1024)
    tm_half = _largest_tile(m, 512)
    x2d = x.reshape(m, d)

    col_b = d_conv
    col_qkv = 2 * d_conv
    col_qm = col_qkv + 3 * ATTN_WIDTH
    col_gate = col_qm + mem_width
    h = _rmsnorm(x2d, p["g_mix"], tm_half)
    u = _proj_glu(h, p["w_in"], 0, col_b, d_conv, tm, 512)
    tn_qkv = 512
    ones = jnp.ones((ATTN_WIDTH,), F32)
    qkv_gain = jnp.concatenate([jnp.tile(p["g_q"], ATTN_WIDTH // HEAD_DIM),
                                jnp.tile(p["g_k"], ATTN_WIDTH // HEAD_DIM), ones]).reshape(1, 3 * ATTN_WIDTH)
    qkv = _proj_headnorm(h, p["w_in"], col_qkv, 3 * ATTN_WIDTH, qkv_gain, HEAD_DIM,
                         2 * ATTN_WIDTH // tn_qkv, tm, tn_qkv, F32)
    mem_head_dim = mem_width // MEM_HEADS
    qm_gain = jnp.tile(p["g_mq"], MEM_HEADS).reshape(1, mem_width)
    qm = _proj_headnorm(h, p["w_in"], col_qm, mem_width, qm_gain, mem_head_dim, mem_width // 512, tm, 512, F32)

    u3 = u.reshape(b, t, d_conv)
    if sample:
        state_padded = jnp.pad(conv_state, ((0, 0), (CONV_PAD, 0), (0, 0)))
        c = _conv_sample(u3, state_padded, p["w_dw"], p["b_dw"], p["g_cln"], p["b_cln"], nb=4).astype(BF16)
        conv_new = jnp.concatenate([conv_state[:, t:], u3], axis=1)
    else:
        c = _conv_prompt(u3, p["w_dw"], p["b_dw"], p["g_cln"], p["b_cln"], tr=256)
        conv_new = u3[:, t - (CONV_WIDTH - 1):]
    c = c.reshape(m, d_conv)

    qkv3 = qkv.reshape(b, t, 3 * ATTN_WIDTH)
    if sample:
        flat_states = [s.reshape(b, s.shape[1], GROUP_WIDTH) for s in win_states]
        for s, (window, _) in zip(flat_states[::2], DIL_GROUPS):
            assert s.shape[1] == window, "window buffers shorter than the window are not supported"
        o_attn, new_states = _attn_sample(qkv3, flat_states)
        o_attn = o_attn.reshape(m, GROUP_WIDTH).astype(BF16)
        win_new = [s.reshape(b, s.shape[1], HEADS_PER_GROUP, HEAD_DIM) for s in new_states]
    else:
        outs, lses, win_new = [], [], []
        for g, (window, dil) in enumerate(DIL_GROUPS):
            o_g, lse_g = _attn_prompt_group(qkv3, g, dil)
            outs.append(o_g.reshape(m, GROUP_WIDTH))
            lses.append(lse_g.reshape(m, GROUP_WIDTH))
            keep = min(window, t)
            for which in (1, 2):
                c0 = which * ATTN_WIDTH + g * GROUP_WIDTH
                win_new.append(qkv3[:, t - keep:, c0:c0 + GROUP_WIDTH].reshape(b, keep, HEADS_PER_GROUP, HEAD_DIM))
        o_attn = _mix_groups(outs, lses, tm)

    o_mem = _mem_attn(qm.reshape(b, t, mem_width), mem_k, mem_v, _largest_tile(t, 512), F32 if sample else BF16)
    o_mem = o_mem.reshape(m, mem_width).astype(BF16)

    merged = _merge(h, c, o_attn, o_mem, p["w_in"], col_gate, p["w_conv_out"], p["w_attn_out"], p["w_mem_out"],
                    tm_half, 256)
    x2 = _proj_residual(merged, p["w_o"], x2d, tm_half, 1024)

    h2 = _rmsnorm(x2, p["g_ffn"], tm_half)
    if sample:
        prev_rows = jnp.pad(ffn_state, ((0, 0), (SUBLANES - (FFN_CONV_WIDTH - 1), 0), (0, 0))).reshape(m, 2 * d_ff)
        act, up = _ffn_up(h2, p["w_up"], p["w_ffn_dw"], p["b_ffn_dw"], tm_half, 512, prev_rows=prev_rows)
        up = jnp.concatenate([up[0], up[1]], axis=-1).reshape(b, t, 2 * d_ff)
        ffn_new = up[:, t - (FFN_CONV_WIDTH - 1):]
    else:
        tiles_per_seq = t // tm_half
        act, last = _ffn_up(h2, p["w_up"], p["w_ffn_dw"], p["b_ffn_dw"], tm_half, 512, tiles_per_seq=tiles_per_seq)
        last = last[tiles_per_seq - 1::tiles_per_seq]
        last = jnp.transpose(last, (0, 2, 1, 3)).reshape(b, SUBLANES, 2 * d_ff)
        ffn_new = last[:, SUBLANES - (FFN_CONV_WIDTH - 1):]
    x3 = _proj_residual(act, p["w_down"], x2, _largest_tile(m, 512), 512)
    return x3.reshape(b, t, d), (conv_new, *win_new, ffn_new)


def _memory_kv(mem, g_mem, w_mem_k, w_mem_v, g_mk):
    b, mt, d = mem.shape
    width = w_mem_k.shape[1]
    tm = _largest_tile(b * mt, 512)
    hm = _rmsnorm(mem.reshape(b * mt, d), g_mem, tm)
    gain = jnp.tile(g_mk, MEM_HEADS).reshape(1, width)
    k = _proj_headnorm(hm, w_mem_k, 0, width, gain, width // MEM_HEADS, width // 512, tm, 512, F32)
    v = _proj_plain(hm, w_mem_v, 0, width, tm, 512)
    return k.reshape(b, mt, width), v.reshape(b, mt, width)


def kernel(x_prompt, x_sample, cache_mem_k, cache_mem_v, state_conv, state_win1_k, state_win1_v, state_win2_k, state_win2_v, state_win3_k, state_win3_v, state_ffn_conv, mem_prompt, g_mix, w_in, w_dw, b_dw, g_cln, b_cln, w_conv_out, g_q, g_k, w_attn_out, g_mem, w_mem_k, w_mem_v, g_mq, g_mk, w_mem_out, w_o, g_ffn, w_up, w_ffn_dw, b_ffn_dw, w_down):
    depth = g_mix.shape[0]
    xp, xs = x_prompt, x_sample
    new_p, new_s = [], []
    for l in range(depth):
        p = {"g_mix": g_mix[l], "w_in": w_in[l], "w_dw": w_dw[l], "b_dw": b_dw[l], "g_cln": g_cln[l],
             "b_cln": b_cln[l], "w_conv_out": w_conv_out[l], "g_q": g_q[l], "g_k": g_k[l],
             "w_attn_out": w_attn_out[l], "g_mq": g_mq[l], "w_mem_out": w_mem_out[l], "w_o": w_o[l],
             "g_ffn": g_ffn[l], "w_up": w_up[l], "w_ffn_dw": w_ffn_dw[l], "b_ffn_dw": b_ffn_dw[l],
             "w_down": w_down[l]}
        mk_p, mv_p = _memory_kv(mem_prompt, g_mem[l], w_mem_k[l], w_mem_v[l], g_mk[l])
        xp, st_p = _layer(xp, p, mem_k=mk_p, mem_v=mv_p)
        bs, mt = cache_mem_k.shape[1], cache_mem_k.shape[2]
        wins = (state_win1_k[l], state_win1_v[l], state_win2_k[l], state_win2_v[l], state_win3_k[l],
                state_win3_v[l])
        xs, st_s = _layer(xs, p, mem_k=cache_mem_k[l].reshape(bs, mt, -1), mem_v=cache_mem_v[l].reshape(bs, mt, -1),
                          conv_state=state_conv[l], win_states=wins, ffn_state=state_ffn_conv[l])
        mem_shape = (mk_p.shape[0], mk_p.shape[1], MEM_HEADS, mk_p.shape[2] // MEM_HEADS)
        new_p.append(st_p + (mk_p.reshape(mem_shape), mv_p.reshape(mem_shape)))
        new_s.append(st_s)
    sp = [jnp.stack(ts, axis=0) for ts in zip(*new_p)]
    ss = [jnp.stack(ts, axis=0) for ts in zip(*new_s)]
    return (xp, xs, *sp, *ss)
```
